```python
import jax, jax.numpy as jnp
from jax import lax
import numpy as np


D_MODEL = 1024
BATCH = 32
SEQ = 2048
DEPTH = 1

CHUNK = 64
N_META = 16
CONV_WIDTH = D_MODEL
CONV_K = 3
SB_HEADS = 16
SB_HEAD_DIM = 64
SB_WIDTH = SB_HEADS * SB_HEAD_DIM
N_BRANCH = 2
D_FF = -(-8 * D_MODEL // (3 * 256)) * 256
Q_BLOCK = 128
EPS = 1e-6
SPLITS = (CONV_WIDTH, 2 * CONV_WIDTH, 3 * CONV_WIDTH,
          3 * CONV_WIDTH + SB_WIDTH, 3 * CONV_WIDTH + 2 * SB_WIDTH, 3 * CONV_WIDTH + 3 * SB_WIDTH)
IN_COLS = 3 * CONV_WIDTH + 3 * SB_WIDTH + N_BRANCH * D_MODEL

kernel_name = "hybrid_shortconv_stickbreaking_gated_block"


def rmsnorm(x, g):
    xf = x.astype(jnp.float32)
    y = xf * lax.rsqrt(jnp.mean(xf * xf, axis=-1, keepdims=True) + EPS)
    return (y * g.astype(jnp.float32)).astype(x.dtype)


def short_conv(u, w):
    c = u.shape[-1]
    return lax.conv_general_dilated(
        u, w[:, None, :].astype(u.dtype), window_strides=(1,),
        padding=[(CONV_K - 1, 0)],
        dimension_numbers=('NWC', 'WIO', 'NWC'),
        feature_group_count=c)


def stick_breaking(q, k, v):
    lp = q.shape[2]
    scale = SB_HEAD_DIM ** -0.5
    outs = []
    for i in range(lp // Q_BLOCK):
        q0 = i * Q_BLOCK
        kend = q0 + Q_BLOCK
        qb = q[:, :, q0:kend].astype(jnp.float32)
        kb = k[:, :, :kend].astype(jnp.float32)
        vb = v[:, :, :kend].astype(jnp.float32)
        z = jnp.einsum('bhqd,bhkd->bhqk', qb, kb) * scale
        t_idx = q0 + jnp.arange(Q_BLOCK)[:, None]
        s_idx = jnp.arange(kend)[None, :]
        causal = s_idx < t_idx
        log_beta = jax.nn.log_sigmoid(z)
        log_1m_beta = jnp.where(causal, log_beta - z, 0.0)
        after = lax.cumsum(log_1m_beta, axis=3, reverse=True) - log_1m_beta
        a = jnp.where(causal, jnp.exp(log_beta + after), 0.0)
        outs.append(jnp.einsum('bhqk,bhkd->bhqd', a, vb))
    return jnp.concatenate(outs, axis=2).astype(v.dtype)


def setup_inputs(seed: int = 0) -> dict:
    key = jax.random.key(seed)
    ks = jax.random.split(key, 16)
    nrm = jax.random.normal
    x = nrm(ks[0], (BATCH, SEQ, D_MODEL), jnp.float32)
    meta_tokens = nrm(ks[1], (N_META, D_MODEL), jnp.float32)
    norm1_g = 1.0 + 0.02 * nrm(ks[2], (DEPTH, D_MODEL), jnp.float32)
    w_in = nrm(ks[3], (DEPTH, D_MODEL, IN_COLS), jnp.float32) * D_MODEL ** -0.5
    b_gate = 0.02 * nrm(ks[4], (DEPTH, N_BRANCH * D_MODEL), jnp.float32)
    conv_w = nrm(ks[5], (DEPTH, CONV_K, CONV_WIDTH), jnp.float32) * CONV_K ** -0.5
    w_conv_out = nrm(ks[6], (DEPTH, CONV_WIDTH, D_MODEL), jnp.float32) * CONV_WIDTH ** -0.5
    q_norm_g = 1.0 + 0.02 * nrm(ks[7], (DEPTH, SB_HEAD_DIM), jnp.float32)
    k_norm_g = 1.0 + 0.02 * nrm(ks[8], (DEPTH, SB_HEAD_DIM), jnp.float32)
    w_sb_out = nrm(ks[9], (DEPTH, SB_WIDTH, D_MODEL), jnp.float32) * SB_WIDTH ** -0.5
    w_o = nrm(ks[10], (DEPTH, D_MODEL, D_MODEL), jnp.float32) * D_MODEL ** -0.5
    norm2_g = 1.0 + 0.02 * nrm(ks[11], (DEPTH, D_MODEL), jnp.float32)
    w_ffn_in = nrm(ks[12], (DEPTH, D_MODEL, 2 * D_FF), jnp.float32) * D_MODEL ** -0.5
    w_ffn_out = nrm(ks[13], (DEPTH, D_FF, D_MODEL), jnp.float32) * D_FF ** -0.5
    return {"x": x, "meta_tokens": meta_tokens, "norm1_g": norm1_g, "w_in": w_in,
            "b_gate": b_gate, "conv_w": conv_w, "w_conv_out": w_conv_out,
            "q_norm_g": q_norm_g, "k_norm_g": k_norm_g, "w_sb_out": w_sb_out,
            "w_o": w_o, "norm2_g": norm2_g, "w_ffn_in": w_ffn_in, "w_ffn_out": w_ffn_out}


def reference(x, meta_tokens, norm1_g, w_in, b_gate, conv_w, w_conv_out,
              q_norm_g, k_norm_g, w_sb_out, w_o, norm2_g, w_ffn_in, w_ffn_out):
    bsz = x.shape[0]
    meta = jnp.broadcast_to(meta_tokens[None].astype(x.dtype), (bsz, N_META, x.shape[-1]))
    h = jnp.concatenate([meta, x], axis=1)
    seq_len = h.shape[1]
    pad = (-seq_len) % Q_BLOCK
    for l in range(DEPTH):
        n = rmsnorm(h, norm1_g[l])
        proj = n @ w_in[l]
        xb, xc, xu, q, k, v, g = jnp.split(proj, SPLITS, axis=-1)

        y_conv = (xb * short_conv(xc * xu, conv_w[l])) @ w_conv_out[l]

        def heads(t, gain):
            t = t.reshape(bsz, seq_len, SB_HEADS, SB_HEAD_DIM)
            if gain is not None:
                t = rmsnorm(t, gain)
            t = jnp.transpose(t, (0, 2, 1, 3))
            return jnp.pad(t, ((0, 0), (0, 0), (0, pad), (0, 0)))
        o = stick_breaking(heads(q, q_norm_g[l]), heads(k, k_norm_g[l]), heads(v, None))
        o = jnp.transpose(o[:, :, :seq_len], (0, 2, 1, 3)).reshape(bsz, seq_len, SB_WIDTH)
        y_sb = o @ w_sb_out[l]

        gates = jax.nn.sigmoid(g + b_gate[l]).reshape(bsz, seq_len, N_BRANCH, -1)
        merged = gates[:, :, 0] * y_conv + gates[:, :, 1] * y_sb
        h = h + merged @ w_o[l]

        n2 = rmsnorm(h, norm2_g[l])
        gate, up = jnp.split(n2 @ w_ffn_in[l], 2, axis=-1)
        h = h + (jax.nn.silu(gate) * up) @ w_ffn_out[l]
    return h[:, N_META:]
```

```python
import functools

import jax
import jax.numpy as jnp
from jax import lax
from jax.experimental import pallas as pl
from jax.experimental.pallas import tpu as pltpu

D_MODEL = 1024
N_META = 16
CONV_K = 3
SB_HEADS = 16
SB_HEAD_DIM = 64
D_FF = 2816
EPS = 1e-6
N_GROUPS = 8

LANES = 128
SUBLANES = 8
VMEM_LIMIT_BYTES = 52 * 1024 * 1024

TM_IN = 512
TN_IN = 256
TQ = 256
TK = 256
TM_OUT = 512
TF = 1408

F32 = jnp.float32
BF16 = jnp.bfloat16


def _dot(a, b):
    return jnp.dot(a, b, preferred_element_type=F32)


def _dot_nt(a, b):
    return lax.dot_general(a, b, (((1,), (1,)), ((), ())), preferred_element_type=F32)


def _in_proj_kernel(x_ref, g_ref, wxb_ref, wxc_ref, wxu_ref, wq_ref, wk_ref, wv_ref,
                    wg0_ref, wg1_ref, bg0_ref, bg1_ref, cw_ref, wco_ref, qg_ref, kg_ref,
                    seg_ref, halo0_ref,
                    gc_ref, q_ref, k_ref, v_ref, g1_ref, utail_ref,
                    n_scr, acc_scr, g0_scr, halo_scr, *, tiles_per_seq, n_col):
    i = pl.program_id(0)
    j = pl.program_id(1)
    tm = x_ref.shape[0]
    tn = wxb_ref.shape[1]

    @pl.when(j == 0)
    def _():
        x = x_ref[...]
        ms = jnp.mean(x * x, axis=-1, keepdims=True)
        n_scr[...] = (x * lax.rsqrt(ms + EPS) * g_ref[...]).astype(BF16)

    n = n_scr[...]

    u = _dot(n, wxc_ref[...]) * _dot(n, wxu_ref[...])
    @pl.when((i % tiles_per_seq) == 0)
    def _():
        halo_scr[j] = halo0_ref[...]

    prev = halo_scr[j]
    rows = lax.broadcasted_iota(jnp.int32, (tm, tn), 0)
    p1 = prev[SUBLANES - 1:SUBLANES, :]
    p2 = prev[SUBLANES - 2:SUBLANES - 1, :]
    u1 = jnp.where(rows == 0, p1, pltpu.roll(u, 1, 0))
    u2 = jnp.where(rows == 0, p2, jnp.where(rows == 1, p1, pltpu.roll(u, 2, 0)))
    cw = cw_ref[...]
    conv = cw[0:1, :] * u2 + cw[1:2, :] * u1 + cw[2:3, :] * u
    tail = u[tm - SUBLANES:, :]
    halo_scr[j] = tail
    utail_ref[...] = tail
    yc = (_dot(n, wxb_ref[...]) * conv).astype(BF16)
    part = _dot(yc, wco_ref[...])

    @pl.when(j == 0)
    def _():
        acc_scr[...] = part

    @pl.when(j != 0)
    def _():
        acc_scr[...] += part

    def head_norm(t, gain):
        ms = _dot((t * t).astype(BF16), seg_ref[...])
        return t * lax.rsqrt(ms + EPS) * gain

    q = head_norm(_dot(n, wq_ref[...]), qg_ref[...])
    q_ref[...] = (q * (SB_HEAD_DIM ** -0.5)).astype(BF16)
    k_ref[...] = head_norm(_dot(n, wk_ref[...]), kg_ref[...]).astype(BF16)
    v_ref[...] = _dot(n, wv_ref[...]).astype(BF16)

    g0 = jax.nn.sigmoid(_dot(n, wg0_ref[...]) + bg0_ref[...])
    g1 = jax.nn.sigmoid(_dot(n, wg1_ref[...]) + bg1_ref[...])
    g1_ref[...] = g1.astype(BF16)
    col = pl.multiple_of(j * tn, tn)
    g0_scr[:, pl.ds(col, tn)] = g0

    @pl.when(j == n_col - 1)
    def _():
        gc_ref[...] = (g0_scr[...] * acc_scr[...]).astype(BF16)


def _in_proj(x2d, tm, tiles_per_seq, halo0, norm_g, w_in, b_gate, conv_w, w_conv_out,
             q_gain, k_gain, seg):
    rows = x2d.shape[0]
    n_row = rows // tm
    tn = TN_IN
    n_col = D_MODEL // tn

    def w_spec(group):
        return pl.BlockSpec((D_MODEL, tn), lambda i, j, g=group: (0, g * n_col + j))

    def col_spec(r):
        return pl.BlockSpec((r, tn), lambda i, j: (0, j))

    tile_spec = pl.BlockSpec((tm, tn), lambda i, j: (i, j))
    in_specs = [
        pl.BlockSpec((tm, D_MODEL), lambda i, j: (i, 0)),
        pl.BlockSpec((1, D_MODEL), lambda i, j: (0, 0)),
        w_spec(0), w_spec(1), w_spec(2), w_spec(3), w_spec(4), w_spec(5), w_spec(6), w_spec(7),
        col_spec(1),
        pl.BlockSpec((1, tn), lambda i, j: (0, n_col + j)),
        col_spec(CONV_K),
        pl.BlockSpec((tn, D_MODEL), lambda i, j: (j, 0)),
        col_spec(1), col_spec(1),
        pl.BlockSpec((tn, tn), lambda i, j: (0, 0)),
        col_spec(SUBLANES),
    ]
    out_specs = [
        pl.BlockSpec((tm, D_MODEL), lambda i, j: (i, 0)),
        tile_spec, tile_spec, tile_spec, tile_spec,
        pl.BlockSpec((SUBLANES, tn), lambda i, j: (i, j)),
    ]
    out_shape = [
        jax.ShapeDtypeStruct((rows, D_MODEL), BF16),
        jax.ShapeDtypeStruct((rows, D_MODEL), BF16),
        jax.ShapeDtypeStruct((rows, D_MODEL), BF16),
        jax.ShapeDtypeStruct((rows, D_MODEL), BF16),
        jax.ShapeDtypeStruct((rows, D_MODEL), BF16),
        jax.ShapeDtypeStruct((n_row * SUBLANES, D_MODEL), F32),
    ]
    scratch = [
        pltpu.VMEM((tm, D_MODEL), BF16),
        pltpu.VMEM((tm, D_MODEL), F32),
        pltpu.VMEM((tm, D_MODEL), F32),
        pltpu.VMEM((n_col, SUBLANES, tn), F32),
    ]
    kern = functools.partial(_in_proj_kernel, tiles_per_seq=tiles_per_seq, n_col=n_col)
    return pl.pallas_call(
        kern,
        grid=(n_row, n_col),
        in_specs=in_specs,
        out_specs=out_specs,
        out_shape=out_shape,
        scratch_shapes=scratch,
        compiler_params=pltpu.CompilerParams(
            dimension_semantics=("arbitrary", "arbitrary"),
            vmem_limit_bytes=VMEM_LIMIT_BYTES),
        name=f"in_proj_{tm}",
    )(x2d, norm_g, w_in, w_in, w_in, w_in, w_in, w_in, w_in, w_in, b_gate, b_gate,
      conv_w, w_conv_out, q_gain, k_gain, seg, halo0)


def _sb_tile(qm, k_t, v_t, carry, tri, mask):
    z = _dot_nt(qm, k_t)
    sp = jnp.log(1.0 + jnp.exp(-jnp.abs(z)))
    lb = jnp.minimum(z, 0.0) - sp
    l = lb - z
    if mask is not None:
        l = jnp.where(mask, l, 0.0)
    l_hi = l.astype(BF16)
    l_lo = (l - l_hi.astype(F32)).astype(BF16)
    incl = _dot(l_hi, tri) + _dot(l_lo, tri)
    a = jnp.exp(z + incl + carry)
    if mask is not None:
        a = jnp.where(mask, a, 0.0)
    pv = _dot(a.astype(BF16), v_t)
    return pv, carry + incl[:, 0:1]


def _sb_attn_kernel(q_ref, k_ref, v_ref, km_ref, vm_ref, tri_ref, trim_ref, o_ref, acc_scr):
    qi = pl.program_id(2)
    tq = q_ref.shape[0]
    q = q_ref[...]
    lane = lax.broadcasted_iota(jnp.int32, q.shape, 1)
    first = lane < SB_HEAD_DIM
    qm = (jnp.where(first, q, jnp.zeros_like(q)), jnp.where(first, jnp.zeros_like(q), q))
    tri = tri_ref[...]

    r = lax.broadcasted_iota(jnp.int32, (tq, TK), 0)
    c = lax.broadcasted_iota(jnp.int32, (tq, TK), 1)
    mask = c < r
    d0 = pl.multiple_of(qi * TK, TK)
    k_t = k_ref[pl.ds(d0, TK), :]
    v_t = v_ref[pl.ds(d0, TK), :]
    carries = []
    for h in range(2):
        pv, cr = _sb_tile(qm[h], k_t, v_t, jnp.zeros((tq, 1), F32), tri, mask)
        acc_scr[h] = pv
        carries.append(cr)

    def body(step, carry):
        start = pl.multiple_of((qi - 1 - step) * TK, TK)
        k_t = k_ref[pl.ds(start, TK), :]
        v_t = v_ref[pl.ds(start, TK), :]
        out = []
        for h in range(2):
            pv, cr = _sb_tile(qm[h], k_t, v_t, carry[h], tri, None)
            acc_scr[h] += pv
            out.append(cr)
        return tuple(out)

    carries = lax.fori_loop(0, qi, body, tuple(carries))

    km = km_ref[...]
    vm = vm_ref[...]
    trim = trim_ref[...]
    accs = []
    for h in range(2):
        pv, _ = _sb_tile(qm[h], km, vm, carries[h], trim, None)
        accs.append(acc_scr[h] + pv)
    lane_o = lax.broadcasted_iota(jnp.int32, accs[0].shape, 1)
    o_ref[...] = jnp.where(lane_o < SB_HEAD_DIM, accs[0], accs[1]).astype(BF16)


def _sb_attn(q, k, v, k_meta, v_meta, batch, seq):
    n_q = seq // TQ
    n_pairs = SB_HEADS * SB_HEAD_DIM // LANES
    idx = jnp.arange(TK)
    tri = (idx[:, None] >= idx[None, :]).astype(BF16)
    idm = jnp.arange(N_META)
    trim = (idm[:, None] >= idm[None, :]).astype(BF16)
    q_spec = pl.BlockSpec((TQ, LANES), lambda b, p, i: (b * n_q + i, p))
    kv_spec = pl.BlockSpec((seq, LANES), lambda b, p, i: (b, p))
    meta_spec = pl.BlockSpec((N_META, LANES), lambda b, p, i: (0, p))
    return pl.pallas_call(
        _sb_attn_kernel,
        grid=(batch, n_pairs, n_q),
        in_specs=[q_spec, kv_spec, kv_spec, meta_spec, meta_spec,
                  pl.BlockSpec((TK, TK), lambda b, p, i: (0, 0)),
                  pl.BlockSpec((N_META, N_META), lambda b, p, i: (0, 0))],
        out_specs=q_spec,
        out_shape=jax.ShapeDtypeStruct(q.shape, BF16),
        scratch_shapes=[pltpu.VMEM((2, TQ, LANES), F32)],
        compiler_params=pltpu.CompilerParams(
            dimension_semantics=("arbitrary", "arbitrary", "arbitrary"),
            vmem_limit_bytes=VMEM_LIMIT_BYTES),
        name="sb_attn",
    )(q, k, v, k_meta, v_meta, tri, trim)


def _out_ffn_kernel(x_ref, gc_ref, g1_ref, o_ref, wsb_ref, wo_ref, g2_ref,
                    wgate_ref, wup_ref, wdown_ref, out_ref, h_scr, n2_scr):
    f = pl.program_id(1)

    @pl.when(f == 0)
    def _():
        y_sb = _dot(o_ref[...], wsb_ref[...])
        merged = gc_ref[...].astype(F32) + g1_ref[...].astype(F32) * y_sb
        h = x_ref[...] + _dot(merged.astype(BF16), wo_ref[...])
        h_scr[...] = h
        ms = jnp.mean(h * h, axis=-1, keepdims=True)
        n2_scr[...] = (h * lax.rsqrt(ms + EPS) * g2_ref[...]).astype(BF16)

    n2 = n2_scr[...]
    gate = _dot(n2, wgate_ref[...])
    up = _dot(n2, wup_ref[...])
    mid = (gate * jax.nn.sigmoid(gate) * up).astype(BF16)
    h_scr[...] += _dot(mid, wdown_ref[...])

    @pl.when(f == pl.num_programs(1) - 1)
    def _():
        out_ref[...] = h_scr[...]


def _out_ffn(x2d, gc, g1, o, w_sb_out, w_o, norm2_g, w_ffn_in, w_ffn_out):
    rows = x2d.shape[0]
    tm = TM_OUT
    n_f = D_FF // TF
    row_spec = pl.BlockSpec((tm, D_MODEL), lambda i, f: (i, 0))
    once = pl.Buffered(1)
    sq_spec = pl.BlockSpec((D_MODEL, D_MODEL), lambda i, f: (0, 0), pipeline_mode=once)
    return pl.pallas_call(
        _out_ffn_kernel,
        grid=(rows // tm, n_f),
        in_specs=[row_spec, row_spec, row_spec, row_spec, sq_spec, sq_spec,
                  pl.BlockSpec((1, D_MODEL), lambda i, f: (0, 0), pipeline_mode=once),
                  pl.BlockSpec((D_MODEL, TF), lambda i, f: (0, f)),
                  pl.BlockSpec((D_MODEL, TF), lambda i, f: (0, n_f + f)),
                  pl.BlockSpec((TF, D_MODEL), lambda i, f: (f, 0))],
        out_specs=row_spec,
        out_shape=jax.ShapeDtypeStruct((rows, D_MODEL), F32),
        scratch_shapes=[pltpu.VMEM((tm, D_MODEL), F32), pltpu.VMEM((tm, D_MODEL), BF16)],
        compiler_params=pltpu.CompilerParams(
            dimension_semantics=("arbitrary", "arbitrary"),
            vmem_limit_bytes=VMEM_LIMIT_BYTES),
        name="out_ffn",
    )(x2d, gc, g1, o, w_sb_out, w_o, norm2_g, w_ffn_in, w_ffn_in, w_ffn_out)


def kernel(x, meta_tokens, norm1_g, w_in, b_gate, conv_w, w_conv_out, q_norm_g, k_norm_g,
           w_sb_out, w_o, norm2_g, w_ffn_in, w_ffn_out):
    batch, seq, d = x.shape
    assert d == D_MODEL and seq % TQ == 0 and seq % TM_IN == 0 and norm1_g.shape[0] == 1
    x2d = x.reshape(batch * seq, d)

    w_in_b = w_in[0].astype(BF16)
    w_co_b = w_conv_out[0].astype(BF16)
    reps = D_MODEL // SB_HEAD_DIM
    q_gain = jnp.tile(q_norm_g[0], reps)[None, :]
    k_gain = jnp.tile(k_norm_g[0], reps)[None, :]
    head = jnp.arange(TN_IN) // SB_HEAD_DIM
    seg = jnp.where(head[:, None] == head[None, :], 1.0 / SB_HEAD_DIM, 0.0).astype(BF16)
    proj_args = (norm1_g, w_in_b, b_gate, conv_w[0], w_co_b, q_gain, k_gain, seg)

    zero_halo = jnp.zeros((SUBLANES, D_MODEL), F32)
    _, _, k_meta, v_meta, _, u_tail = _in_proj(
        meta_tokens.astype(F32), N_META, 1, zero_halo, *proj_args)

    gc, q, k, v, g1, _ = _in_proj(x2d, TM_IN, seq // TM_IN, u_tail, *proj_args)
    o = _sb_attn(q, k, v, k_meta, v_meta, batch, seq)
    out = _out_ffn(x2d, gc, g1, o, w_sb_out[0].astype(BF16), w_o[0].astype(BF16), norm2_g,
                   w_ffn_in[0].astype(BF16), w_ffn_out[0].astype(BF16))
    return out.reshape(batch, seq, d)
```

```python
import functools
import math

import numpy as np
import jax
import jax.numpy as jnp
from jax import lax
from jax.experimental import pallas as pl
from jax.experimental.pallas import tpu as pltpu

D_MODEL = 1024
N_META = 16
CONV_K = 3
SB_HEADS = 16
SB_HEAD_DIM = 64
D_FF = 2816
EPS = 1e-6
N_GROUPS = 8

LANES = 128
SUBLANES = 8
VMEM_LIMIT_BYTES = 52 * 1024 * 1024

TM_IN = 512
TN_IN = 256
TQ = 256
TK = 256
TM_OUT = 512
TF = 1408

QK_SCALE = SB_HEAD_DIM ** -0.5 * math.log2(math.e)
MASKED = -1e30

F32 = jnp.float32
BF16 = jnp.bfloat16


def _dot(a, b):
    return jnp.dot(a, b, preferred_element_type=F32)


def _dot_nt(a, b):
    return lax.dot_general(a, b, (((1,), (1,)), ((), ())), preferred_element_type=F32)


def _in_proj_kernel(x_ref, g_ref, w_ref, bg0_ref, bg1_ref, cw_ref, wco_ref, qg_ref, kg_ref,
                    seg_ref, halo0_ref,
                    gc_ref, q_ref, k_ref, v_ref, g1_ref, utail_ref,
                    n_scr, acc_scr, g0_scr, halo_scr, *, tiles_per_seq, n_col):
    i = pl.program_id(0)
    j = pl.program_id(1)
    tm = x_ref.shape[0]
    tn = q_ref.shape[1]

    @pl.when(j == 0)
    def _():
        x = x_ref[...]
        ms = jnp.mean(x * x, axis=-1, keepdims=True)
        n_scr[...] = (x * lax.rsqrt(ms + EPS) * g_ref[...]).astype(BF16)
        acc_scr[...] = jnp.zeros_like(acc_scr)

    @pl.when((i % tiles_per_seq) == 0)
    def _():
        halo_scr[j] = halo0_ref[...]

    p = _dot(n_scr[...], w_ref[...])
    xb, xc, xu, pq, pk, pv, pg0, pg1 = (p[:, g * tn:(g + 1) * tn] for g in range(N_GROUPS))

    u = xc * xu
    prev = halo_scr[j]
    rows = lax.broadcasted_iota(jnp.int32, (tm, tn), 0)
    p1 = prev[SUBLANES - 1:SUBLANES, :]
    p2 = prev[SUBLANES - 2:SUBLANES - 1, :]
    u1 = jnp.where(rows == 0, p1, pltpu.roll(u, 1, 0))
    u2 = jnp.where(rows == 0, p2, jnp.where(rows == 1, p1, pltpu.roll(u, 2, 0)))
    cw = cw_ref[...]
    conv = cw[0:1, :] * u2 + cw[1:2, :] * u1 + cw[2:3, :] * u
    tail = u[tm - SUBLANES:, :]
    halo_scr[j] = tail
    utail_ref[...] = tail
    yc = (xb * conv).astype(BF16)
    acc_scr[...] += _dot(yc, wco_ref[...])

    sq = jnp.concatenate([pq * pq, pk * pk], axis=0).astype(BF16)
    ms = _dot(sq, seg_ref[...])
    q = pq * lax.rsqrt(ms[0:tm] + EPS) * qg_ref[...]
    q_ref[...] = (q * QK_SCALE).astype(BF16)
    k_ref[...] = (pk * lax.rsqrt(ms[tm:2 * tm] + EPS) * kg_ref[...]).astype(BF16)
    v_ref[...] = pv.astype(BF16)

    g0 = jax.nn.sigmoid(pg0 + bg0_ref[...])
    g1 = jax.nn.sigmoid(pg1 + bg1_ref[...])
    g1_ref[...] = g1.astype(BF16)
    col = pl.multiple_of(j * tn, tn)
    g0_scr[:, pl.ds(col, tn)] = g0

    @pl.when(j == n_col - 1)
    def _():
        gc_ref[...] = (g0_scr[...] * acc_scr[...]).astype(BF16)


def _in_proj(x2d, tm, tiles_per_seq, halo0, norm_g, w_in, b_gate, conv_w, w_conv_out,
             q_gain, k_gain, seg):
    rows = x2d.shape[0]
    n_row = rows // tm
    tn = TN_IN
    n_col = D_MODEL // tn

    def col_spec(r):
        return pl.BlockSpec((r, tn), lambda i, j: (0, j))

    tile_spec = pl.BlockSpec((tm, tn), lambda i, j: (i, j))
    in_specs = [
        pl.BlockSpec((tm, D_MODEL), lambda i, j: (i, 0)),
        pl.BlockSpec((1, D_MODEL), lambda i, j: (0, 0)),
        pl.BlockSpec((D_MODEL, N_GROUPS * tn), lambda i, j: (0, j)),
        col_spec(1),
        pl.BlockSpec((1, tn), lambda i, j: (0, n_col + j)),
        col_spec(CONV_K),
        pl.BlockSpec((tn, D_MODEL), lambda i, j: (j, 0)),
        col_spec(1), col_spec(1),
        pl.BlockSpec((tn, tn), lambda i, j: (0, 0)),
        col_spec(SUBLANES),
    ]
    out_specs = [
        pl.BlockSpec((tm, D_MODEL), lambda i, j: (i, 0)),
        tile_spec, tile_spec, tile_spec, tile_spec,
        pl.BlockSpec((SUBLANES, tn), lambda i, j: (i, j)),
    ]
    out_shape = [
        jax.ShapeDtypeStruct((rows, D_MODEL), BF16),
        jax.ShapeDtypeStruct((rows, D_MODEL), BF16),
        jax.ShapeDtypeStruct((rows, D_MODEL), BF16),
        jax.ShapeDtypeStruct((rows, D_MODEL), BF16),
        jax.ShapeDtypeStruct((rows, D_MODEL), BF16),
        jax.ShapeDtypeStruct((n_row * SUBLANES, D_MODEL), F32),
    ]
    scratch = [
        pltpu.VMEM((tm, D_MODEL), BF16),
        pltpu.VMEM((tm, D_MODEL), F32),
        pltpu.VMEM((tm, D_MODEL), F32),
        pltpu.VMEM((n_col, SUBLANES, tn), F32),
    ]
    kern = functools.partial(_in_proj_kernel, tiles_per_seq=tiles_per_seq, n_col=n_col)
    return pl.pallas_call(
        kern,
        grid=(n_row, n_col),
        in_specs=in_specs,
        out_specs=out_specs,
        out_shape=out_shape,
        scratch_shapes=scratch,
        compiler_params=pltpu.CompilerParams(
            dimension_semantics=("arbitrary", "arbitrary"),
            vmem_limit_bytes=VMEM_LIMIT_BYTES),
        name=f"in_proj_{tm}",
    )(x2d, norm_g, w_in, b_gate, b_gate, conv_w, w_conv_out, q_gain, k_gain, seg, halo0)


BIAS_OPEN, BIAS_DIAG, BIAS_META = 0, 1, 2
SB_STAGES = 5
SB_UNROLL = 12
SB_PAIRS = 4
SB_ROWS = 32


def _sb_schedule(n_q, seq):
    qrow, krow, kind, first = [], [], [], []
    for pair in range(SB_PAIRS):
        for qi in range(n_q):
            for t in range(qi + 2):
                qrow.append(pair * seq + qi * TQ)
                first.append(1 if t == 0 else 0)
                if t == 0:
                    krow.append(pair * (seq + TK) + qi * TK)
                    kind.append(BIAS_DIAG)
                elif t <= qi:
                    krow.append(pair * (seq + TK) + (qi - t) * TK)
                    kind.append(BIAS_OPEN)
                else:
                    krow.append(pair * (seq + TK) + seq)
                    kind.append(BIAS_META)
    tbl = np.array([qrow, krow, kind, first], np.int32)
    pad = SB_STAGES - 1
    return np.concatenate([np.repeat(tbl[:, :1], pad, 1), tbl, np.repeat(tbl[:, -1:], pad, 1)], 1)


def _sb_attn_kernel(tbl_ref, q_ref, k_ref, v_ref, km_ref, vm_ref, bias_ref, tri_ref, o_ref,
                    qm_scr, k_all, v_all, o_scr, z_buf, zb_buf, l_buf, i_buf, a_buf, c_scr,
                    acc_scr, *, n_tiles):
    seq = q_ref.shape[0]
    kv_rows = seq + TK
    n_tbl = n_tiles + 2 * (SB_STAGES - 1)

    def tile_info(t):
        j = t + (SB_STAGES - 1)
        return (pl.multiple_of(tbl_ref[0 * n_tbl + j], TQ),
                pl.multiple_of(tbl_ref[1 * n_tbl + j], TK),
                tbl_ref[2 * n_tbl + j],
                tbl_ref[3 * n_tbl + j])

    def m1(t, u):
        qrow, krow, _, _ = tile_info(t)
        k_t = k_all[pl.ds(krow, TK), :]
        q2 = jnp.concatenate([qm_scr[0, pl.ds(qrow, TQ), :], qm_scr[1, pl.ds(qrow, TQ), :]], axis=0)
        z = _dot_nt(q2, k_t)
        z_buf[u % 2, 0] = z[0:TQ]
        z_buf[u % 2, 1] = z[TQ:2 * TQ]

    def v1(t, u):
        _, _, kind, _ = tile_info(t)
        bias = bias_ref[kind]
        for h in range(2):
            z = z_buf[u % 2, h] + bias
            nl = jnp.maximum(z, 0.0) + jnp.log2(1.0 + jnp.exp2(-jnp.abs(z)))
            zb_buf[u % 4, h] = z
            l_buf[u % 2, h] = nl.astype(BF16)

    def m2(t, u):
        nl2 = jnp.concatenate([l_buf[u % 2, 0], l_buf[u % 2, 1]], axis=0)
        incl = _dot(nl2, tri_ref[...])
        i_buf[u % 2, 0] = incl[0:TQ]
        i_buf[u % 2, 1] = incl[TQ:2 * TQ]

    def v2(t, u):
        _, _, _, first = tile_info(t)
        for h in range(2):
            slot = h + first * (2 - h)
            for r0 in range(0, TQ, SB_ROWS):
                rows = slice(r0, r0 + SB_ROWS)
                carry = c_scr[slot, rows, :]
                for c0 in range(0, TK, LANES):
                    a = jnp.exp2(zb_buf[u % 4, h, rows, c0:c0 + LANES]
                                 - i_buf[u % 2, h, rows, c0:c0 + LANES] - carry)
                    a_buf[u % 2, rows, h * TK + c0:h * TK + c0 + LANES] = a.astype(BF16)
                c_scr[h, rows, :] = carry + jnp.broadcast_to(i_buf[u % 2, h, rows, 0:1],
                                                             (SB_ROWS, LANES))

    def m3(t, u):
        qrow, krow, _, first = tile_info(t)
        v2_t = jnp.concatenate([v_all[0, pl.ds(krow, TK), :], v_all[1, pl.ds(krow, TK), :]], axis=0)
        acc = _dot(a_buf[u % 2], v2_t) + acc_scr[first]
        acc_scr[0] = acc
        o_scr[pl.ds(qrow, TQ), :] = acc.astype(BF16)

    @pl.when((pl.program_id(0) == 0) & (pl.program_id(1) == 0))
    def _():
        for buf in (z_buf, zb_buf, l_buf, i_buf, a_buf):
            buf[...] = jnp.zeros_like(buf)

    head0 = lax.broadcasted_iota(jnp.int32, (1, LANES), 1) < SB_HEAD_DIM
    for p in range(SB_PAIRS):
        lanes = slice(p * LANES, (p + 1) * LANES)
        q = q_ref[:, lanes]
        qm_scr[0, p * seq:(p + 1) * seq, :] = jnp.where(head0, q, jnp.zeros_like(q))
        qm_scr[1, p * seq:(p + 1) * seq, :] = jnp.where(head0, jnp.zeros_like(q), q)
        k_all[p * kv_rows:p * kv_rows + seq, :] = k_ref[:, lanes]
        k_all[p * kv_rows + seq:(p + 1) * kv_rows, :] = km_ref[:, lanes]
        for src, r0, r1 in ((v_ref, 0, seq), (vm_ref, seq, kv_rows)):
            v = src[:, lanes]
            v_all[0, p * kv_rows + r0:p * kv_rows + r1, :] = jnp.where(head0, v, jnp.zeros_like(v))
            v_all[1, p * kv_rows + r0:p * kv_rows + r1, :] = jnp.where(head0, jnp.zeros_like(v), v)
    c_scr[...] = jnp.zeros_like(c_scr)
    acc_scr[...] = jnp.zeros_like(acc_scr)

    def body(it, _):
        s0 = it * SB_UNROLL
        for u in range(SB_UNROLL):
            m1(s0 + u, u)
            v1(s0 + u - 1, u + 3)
            m2(s0 + u - 2, u)
            v2(s0 + u - 3, u + 1)
            m3(s0 + u - 4, u)
        return 0

    lax.fori_loop(0, (n_tiles + SB_STAGES - 1) // SB_UNROLL, body, 0)
    for p in range(SB_PAIRS):
        o_ref[:, p * LANES:(p + 1) * LANES] = o_scr[p * seq:(p + 1) * seq, :]


def _sb_attn(q, k, v, k_meta, v_meta, batch, seq):
    n_q = seq // TQ
    n_groups = SB_HEADS * SB_HEAD_DIM // (LANES * SB_PAIRS)
    width = LANES * SB_PAIRS
    sched = _sb_schedule(n_q, seq)
    n_tiles = sched.shape[1] - 2 * (SB_STAGES - 1)
    assert (n_tiles + SB_STAGES - 1) % SB_UNROLL == 0
    tbl = jnp.asarray(sched.reshape(-1))

    r = np.arange(TQ)[:, None]
    c = np.arange(TK)[None, :]
    bias = np.zeros((3, TQ, TK), np.float32)
    bias[BIAS_DIAG] = np.where(c < r, 0.0, MASKED)
    bias[BIAS_META] = np.where(c < N_META, 0.0, MASKED) + 0.0 * r
    tri = (np.arange(TK)[:, None] >= c).astype(np.float32)
    pad = ((0, TK - N_META), (0, 0))
    k_meta = jnp.pad(k_meta, pad)
    v_meta = jnp.pad(v_meta, pad)

    blk = pl.BlockSpec((seq, width), lambda b, p, t: (b, p))
    meta_spec = pl.BlockSpec((TK, width), lambda b, p, t: (0, p))
    grid_spec = pltpu.PrefetchScalarGridSpec(
        num_scalar_prefetch=1,
        grid=(batch, n_groups),
        in_specs=[blk, blk, blk, meta_spec, meta_spec,
                  pl.BlockSpec((3, TQ, TK), lambda b, p, t: (0, 0, 0)),
                  pl.BlockSpec((TK, TK), lambda b, p, t: (0, 0))],
        out_specs=blk,
        scratch_shapes=[
            pltpu.VMEM((2, SB_PAIRS * seq, LANES), BF16),
            pltpu.VMEM((SB_PAIRS * (seq + TK), LANES), BF16),
            pltpu.VMEM((2, SB_PAIRS * (seq + TK), LANES), BF16),
            pltpu.VMEM((SB_PAIRS * seq, LANES), BF16),
            pltpu.VMEM((2, 2, TQ, TK), F32),
            pltpu.VMEM((4, 2, TQ, TK), F32),
            pltpu.VMEM((2, 2, TQ, TK), BF16),
            pltpu.VMEM((2, 2, TQ, TK), F32),
            pltpu.VMEM((2, TQ, 2 * TK), BF16),
            pltpu.VMEM((3, TQ, LANES), F32),
            pltpu.VMEM((2, TQ, LANES), F32),
        ])
    return pl.pallas_call(
        functools.partial(_sb_attn_kernel, n_tiles=n_tiles),
        grid_spec=grid_spec,
        out_shape=jax.ShapeDtypeStruct(q.shape, BF16),
        compiler_params=pltpu.CompilerParams(
            dimension_semantics=("arbitrary", "arbitrary"),
            vmem_limit_bytes=VMEM_LIMIT_BYTES),
        name="sb_attn",
    )(tbl, q, k, v, k_meta, v_meta, jnp.asarray(bias), jnp.asarray(tri, BF16))


def _out_ffn_kernel(x_ref, gc_ref, g1_ref, o_ref, wsb_ref, wo_ref, g2_ref,
                    wgate_ref, wup_ref, wdown_ref, out_ref, h_scr, n2_scr):
    f = pl.program_id(1)

    @pl.when(f == 0)
    def _():
        y_sb = _dot(o_ref[...], wsb_ref[...])
        merged = gc_ref[...].astype(F32) + g1_ref[...].astype(F32) * y_sb
        h = x_ref[...] + _dot(merged.astype(BF16), wo_ref[...])
        h_scr[...] = h
        ms = jnp.mean(h * h, axis=-1, keepdims=True)
        n2_scr[...] = (h * lax.rsqrt(ms + EPS) * g2_ref[...]).astype(BF16)

    n2 = n2_scr[...]
    gate = _dot(n2, wgate_ref[...])
    up = _dot(n2, wup_ref[...])
    mid = (gate * jax.nn.sigmoid(gate) * up).astype(BF16)
    h_scr[...] += _dot(mid, wdown_ref[...])

    @pl.when(f == pl.num_programs(1) - 1)
    def _():
        out_ref[...] = h_scr[...]


def _out_ffn(x2d, gc, g1, o, w_sb_out, w_o, norm2_g, w_ffn_in, w_ffn_out):
    rows = x2d.shape[0]
    tm = TM_OUT
    n_f = D_FF // TF
    row_spec = pl.BlockSpec((tm, D_MODEL), lambda i, f: (i, 0))
    once = pl.Buffered(1)
    sq_spec = pl.BlockSpec((D_MODEL, D_MODEL), lambda i, f: (0, 0), pipeline_mode=once)
    return pl.pallas_call(
        _out_ffn_kernel,
        grid=(rows // tm, n_f),
        in_specs=[row_spec, row_spec, row_spec, row_spec, sq_spec, sq_spec,
                  pl.BlockSpec((1, D_MODEL), lambda i, f: (0, 0), pipeline_mode=once),
                  pl.BlockSpec((D_MODEL, TF), lambda i, f: (0, f)),
                  pl.BlockSpec((D_MODEL, TF), lambda i, f: (0, n_f + f)),
                  pl.BlockSpec((TF, D_MODEL), lambda i, f: (f, 0))],
        out_specs=row_spec,
        out_shape=jax.ShapeDtypeStruct((rows, D_MODEL), F32),
        scratch_shapes=[pltpu.VMEM((tm, D_MODEL), F32), pltpu.VMEM((tm, D_MODEL), BF16)],
        compiler_params=pltpu.CompilerParams(
            dimension_semantics=("arbitrary", "arbitrary"),
            vmem_limit_bytes=VMEM_LIMIT_BYTES),
        name="out_ffn",
    )(x2d, gc, g1, o, w_sb_out, w_o, norm2_g, w_ffn_in, w_ffn_in, w_ffn_out)


def kernel(x, meta_tokens, norm1_g, w_in, b_gate, conv_w, w_conv_out, q_norm_g, k_norm_g,
           w_sb_out, w_o, norm2_g, w_ffn_in, w_ffn_out):
    batch, seq, d = x.shape
    assert d == D_MODEL and seq % TQ == 0 and seq % TM_IN == 0 and norm1_g.shape[0] == 1
    x2d = x.reshape(batch * seq, d)

    n_col = D_MODEL // TN_IN
    w_in_b = (w_in[0].astype(BF16).reshape(D_MODEL, N_GROUPS, n_col, TN_IN)
              .transpose(0, 2, 1, 3).reshape(D_MODEL, N_GROUPS * D_MODEL))
    w_co_b = w_conv_out[0].astype(BF16)
    reps = D_MODEL // SB_HEAD_DIM
    q_gain = jnp.tile(q_norm_g[0], reps)[None, :]
    k_gain = jnp.tile(k_norm_g[0], reps)[None, :]
    head = jnp.arange(TN_IN) // SB_HEAD_DIM
    seg = jnp.where(head[:, None] == head[None, :], 1.0 / SB_HEAD_DIM, 0.0).astype(BF16)
    proj_args = (norm1_g, w_in_b, b_gate, conv_w[0], w_co_b, q_gain, k_gain, seg)

    zero_halo = jnp.zeros((SUBLANES, D_MODEL), F32)
    _, _, k_meta, v_meta, _, u_tail = _in_proj(
        meta_tokens.astype(F32), N_META, 1, zero_halo, *proj_args)

    gc, q, k, v, g1, _ = _in_proj(x2d, TM_IN, seq // TM_IN, u_tail, *proj_args)
    o = _sb_attn(q, k, v, k_meta, v_meta, batch, seq)
    out = _out_ffn(x2d, gc, g1, o, w_sb_out[0].astype(BF16), w_o[0].astype(BF16), norm2_g,
                   w_ffn_in[0].astype(BF16), w_ffn_out[0].astype(BF16))
    return out.reshape(batch, seq, d)
```

```python
import functools
import math

import numpy as np
import jax
import jax.numpy as jnp
from jax import lax
from jax.experimental import pallas as pl
from jax.experimental.pallas import tpu as pltpu

D_MODEL = 1024
N_META = 16
CONV_K = 3
SB_HEADS = 16
SB_HEAD_DIM = 64
D_FF = 2816
EPS = 1e-6
N_GROUPS = 8

LANES = 128
SUBLANES = 8
VMEM_LIMIT_BYTES = 52 * 1024 * 1024

TM_IN = 512
TN_IN = 256
TQ = 256
TK = 256
TM_OUT = 512
TF = 1408

QK_SCALE = SB_HEAD_DIM ** -0.5 * math.log2(math.e)
MASKED = -1e30

F32 = jnp.float32
BF16 = jnp.bfloat16


def _dot(a, b):
    return jnp.dot(a, b, preferred_element_type=F32)


def _dot_nt(a, b):
    return lax.dot_general(a, b, (((1,), (1,)), ((), ())), preferred_element_type=F32)


def _in_proj_kernel(x_ref, g_ref, w_ref, bg0_ref, bg1_ref, cw_ref, wco_ref, qg_ref, kg_ref,
                    seg_ref, halo0_ref,
                    gc_ref, q_ref, k_ref, v_ref, g1_ref, utail_ref,
                    n_scr, acc_scr, g0_scr, halo_scr, *, tiles_per_seq, n_col):
    i = pl.program_id(0)
    j = pl.program_id(1)
    tm = x_ref.shape[0]
    tn = q_ref.shape[1]

    @pl.when(j == 0)
    def _():
        x = x_ref[...]
        ms = jnp.mean(x * x, axis=-1, keepdims=True)
        n_scr[...] = (x * lax.rsqrt(ms + EPS) * g_ref[...]).astype(BF16)
        acc_scr[...] = jnp.zeros_like(acc_scr)

    @pl.when((i % tiles_per_seq) == 0)
    def _():
        halo_scr[j] = halo0_ref[...]

    p = _dot(n_scr[...], w_ref[...])
    xb, xc, xu, pq, pk, pv, pg0, pg1 = (p[:, g * tn:(g + 1) * tn] for g in range(N_GROUPS))

    u = xc * xu
    prev = halo_scr[j]
    rows = lax.broadcasted_iota(jnp.int32, (tm, tn), 0)
    p1 = prev[SUBLANES - 1:SUBLANES, :]
    p2 = prev[SUBLANES - 2:SUBLANES - 1, :]
    u1 = jnp.where(rows == 0, p1, pltpu.roll(u, 1, 0))
    u2 = jnp.where(rows == 0, p2, jnp.where(rows == 1, p1, pltpu.roll(u, 2, 0)))
    cw = cw_ref[...]
    conv = cw[0:1, :] * u2 + cw[1:2, :] * u1 + cw[2:3, :] * u
    tail = u[tm - SUBLANES:, :]
    halo_scr[j] = tail
    utail_ref[...] = tail
    yc = (xb * conv).astype(BF16)
    acc_scr[...] += _dot(yc, wco_ref[...])

    sq = jnp.concatenate([pq * pq, pk * pk], axis=0).astype(BF16)
    ms = _dot(sq, seg_ref[...])
    q = pq * lax.rsqrt(ms[0:tm] + EPS) * qg_ref[...]
    q_ref[...] = (q * QK_SCALE).astype(BF16)
    k_ref[...] = (pk * lax.rsqrt(ms[tm:2 * tm] + EPS) * kg_ref[...]).astype(BF16)
    v_ref[...] = pv.astype(BF16)

    g0 = jax.nn.sigmoid(pg0 + bg0_ref[...])
    g1 = jax.nn.sigmoid(pg1 + bg1_ref[...])
    g1_ref[...] = g1.astype(BF16)
    col = pl.multiple_of(j * tn, tn)
    g0_scr[:, pl.ds(col, tn)] = g0

    @pl.when(j == n_col - 1)
    def _():
        gc_ref[...] = (g0_scr[...] * acc_scr[...]).astype(BF16)


def _in_proj(x2d, tm, tiles_per_seq, halo0, norm_g, w_in, b_gate, conv_w, w_conv_out,
             q_gain, k_gain, seg):
    rows = x2d.shape[0]
    n_row = rows // tm
    tn = TN_IN
    n_col = D_MODEL // tn

    def col_spec(r):
        return pl.BlockSpec((r, tn), lambda i, j: (0, j))

    tile_spec = pl.BlockSpec((tm, tn), lambda i, j: (i, j))
    in_specs = [
        pl.BlockSpec((tm, D_MODEL), lambda i, j: (i, 0)),
        pl.BlockSpec((1, D_MODEL), lambda i, j: (0, 0)),
        pl.BlockSpec((D_MODEL, N_GROUPS * tn), lambda i, j: (0, j)),
        col_spec(1),
        pl.BlockSpec((1, tn), lambda i, j: (0, n_col + j)),
        col_spec(CONV_K),
        pl.BlockSpec((tn, D_MODEL), lambda i, j: (j, 0)),
        col_spec(1), col_spec(1),
        pl.BlockSpec((tn, tn), lambda i, j: (0, 0)),
        col_spec(SUBLANES),
    ]
    out_specs = [
        pl.BlockSpec((tm, D_MODEL), lambda i, j: (i, 0)),
        tile_spec, tile_spec, tile_spec, tile_spec,
        pl.BlockSpec((SUBLANES, tn), lambda i, j: (i, j)),
    ]
    out_shape = [
        jax.ShapeDtypeStruct((rows, D_MODEL), BF16),
        jax.ShapeDtypeStruct((rows, D_MODEL), BF16),
        jax.ShapeDtypeStruct((rows, D_MODEL), BF16),
        jax.ShapeDtypeStruct((rows, D_MODEL), BF16),
        jax.ShapeDtypeStruct((rows, D_MODEL), BF16),
        jax.ShapeDtypeStruct((n_row * SUBLANES, D_MODEL), F32),
    ]
    scratch = [
        pltpu.VMEM((tm, D_MODEL), BF16),
        pltpu.VMEM((tm, D_MODEL), F32),
        pltpu.VMEM((tm, D_MODEL), F32),
        pltpu.VMEM((n_col, SUBLANES, tn), F32),
    ]
    kern = functools.partial(_in_proj_kernel, tiles_per_seq=tiles_per_seq, n_col=n_col)
    return pl.pallas_call(
        kern,
        grid=(n_row, n_col),
        in_specs=in_specs,
        out_specs=out_specs,
        out_shape=out_shape,
        scratch_shapes=scratch,
        compiler_params=pltpu.CompilerParams(
            dimension_semantics=("arbitrary", "arbitrary"),
            vmem_limit_bytes=VMEM_LIMIT_BYTES),
        name=f"in_proj_{tm}",
    )(x2d, norm_g, w_in, b_gate, b_gate, conv_w, w_conv_out, q_gain, k_gain, seg, halo0)


BIAS_OPEN, BIAS_DIAG, BIAS_META = 0, 1, 2
SB_STAGES = 5
SB_UNROLL = 36
SB_PAIRS = 4
SB_ROWS = 32
SB_HOIST_RANGE = 35


def _sb_schedule(n_q, seq):
    qrow, krow, kind, first = [], [], [], []
    for pair in range(SB_PAIRS):
        for qi in range(n_q):
            for t in range(qi + 2):
                qrow.append(pair * seq + qi * TQ)
                first.append(1 if t == 0 else 0)
                if t == 0:
                    krow.append(pair * (seq + TK) + qi * TK)
                    kind.append(BIAS_DIAG)
                elif t <= qi:
                    krow.append(pair * (seq + TK) + (qi - t) * TK)
                    kind.append(BIAS_OPEN)
                else:
                    krow.append(pair * (seq + TK) + seq)
                    kind.append(BIAS_META)
    tbl = np.array([qrow, krow, kind, first], np.int32)
    pad = SB_STAGES - 1
    return np.concatenate([np.repeat(tbl[:, :1], pad, 1), tbl, np.repeat(tbl[:, -1:], pad, 1)], 1)


def _sb_attn_kernel(tbl_ref, q_ref, k_ref, v_ref, km_ref, vm_ref, bias_ref, tri_ref, o_ref,
                    qm_scr, k_all, v_all, o_scr, z_buf, l_buf, i_buf, a_buf, c_scr,
                    acc_scr, *, n_tiles):
    seq = q_ref.shape[0]
    kv_rows = seq + TK
    n_tbl = n_tiles + 2 * (SB_STAGES - 1)

    def tile_info(t):
        j = t + (SB_STAGES - 1)
        return (pl.multiple_of(tbl_ref[0 * n_tbl + j], TQ),
                pl.multiple_of(tbl_ref[1 * n_tbl + j], TK),
                tbl_ref[2 * n_tbl + j],
                tbl_ref[3 * n_tbl + j])

    def m1(t, u):
        qrow, krow, kind, _ = tile_info(t)
        k_t = k_all[pl.ds(krow, TK), :]
        q2 = jnp.concatenate([qm_scr[0, pl.ds(qrow, TQ), :], qm_scr[1, pl.ds(qrow, TQ), :]], axis=0)
        bias = bias_ref[kind]
        z = _dot_nt(q2, k_t) + jnp.concatenate([bias, bias], axis=0)
        z_buf[u % 4, 0] = z[0:TQ]
        z_buf[u % 4, 1] = z[TQ:2 * TQ]

    def v1(t, u):
        for h in range(2):
            z = z_buf[u % 4, h]
            nl = jnp.maximum(z, 0.0) + jnp.log2(1.0 + jnp.exp2(-jnp.abs(z)))
            l_buf[u % 2, h] = nl.astype(BF16)

    def m2(t, u):
        nl2 = jnp.concatenate([l_buf[u % 2, 0], l_buf[u % 2, 1]], axis=0)
        incl = _dot(nl2, tri_ref[...])
        i_buf[u % 2, 0] = incl[0:TQ]
        i_buf[u % 2, 1] = incl[TQ:2 * TQ]

    def v2(t, u):
        _, _, _, first = tile_info(t)
        for h in range(2):
            slot = h + first * (2 - h)
            for r0 in range(0, TQ, SB_ROWS):
                rows = slice(r0, r0 + SB_ROWS)
                carry = c_scr[slot, rows, :]
                for c0 in range(0, TK, LANES):
                    a = jnp.exp2(z_buf[u % 4, h, rows, c0:c0 + LANES]
                                 - i_buf[u % 2, h, rows, c0:c0 + LANES] - carry)
                    a_buf[u % 2, rows, h * TK + c0:h * TK + c0 + LANES] = a.astype(BF16)
                c_scr[h, rows, :] = carry + jnp.broadcast_to(i_buf[u % 2, h, rows, 0:1],
                                                             (SB_ROWS, LANES))

    def m3(t, u):
        qrow, krow, _, first = tile_info(t)
        acc = acc_scr[first]
        for h in range(2):
            acc = acc + _dot(a_buf[u % 2, :, h * TK:(h + 1) * TK], v_all[h, pl.ds(krow, TK), :])
        acc_scr[0] = acc
        o_scr[pl.ds(qrow, TQ), :] = acc.astype(BF16)

    @pl.when((pl.program_id(0) == 0) & (pl.program_id(1) == 0))
    def _():
        for buf in (z_buf, l_buf, i_buf, a_buf):
            buf[...] = jnp.zeros_like(buf)

    head0 = lax.broadcasted_iota(jnp.int32, (1, LANES), 1) < SB_HEAD_DIM
    for p in range(SB_PAIRS):
        lanes = slice(p * LANES, (p + 1) * LANES)
        q = q_ref[:, lanes]
        qm_scr[0, p * seq:(p + 1) * seq, :] = jnp.where(head0, q, jnp.zeros_like(q))
        qm_scr[1, p * seq:(p + 1) * seq, :] = jnp.where(head0, jnp.zeros_like(q), q)
        k_all[p * kv_rows:p * kv_rows + seq, :] = k_ref[:, lanes]
        k_all[p * kv_rows + seq:(p + 1) * kv_rows, :] = km_ref[:, lanes]
        for src, r0, r1 in ((v_ref, 0, seq), (vm_ref, seq, kv_rows)):
            v = src[:, lanes]
            v_all[0, p * kv_rows + r0:p * kv_rows + r1, :] = jnp.where(head0, v, jnp.zeros_like(v))
            v_all[1, p * kv_rows + r0:p * kv_rows + r1, :] = jnp.where(head0, jnp.zeros_like(v), v)
    c_scr[...] = jnp.zeros_like(c_scr)
    acc_scr[...] = jnp.zeros_like(acc_scr)

    def body(it, _):
        s0 = it * SB_UNROLL
        for u in range(SB_UNROLL):
            m1(s0 + u, u)
            v1(s0 + u - 1, u + 3)
            m2(s0 + u - 2, u)
            v2(s0 + u - 3, u + 1)
            m3(s0 + u - 4, u)
        return 0

    lax.fori_loop(0, (n_tiles + SB_STAGES - 1) // SB_UNROLL, body, 0)
    for p in range(SB_PAIRS):
        o_ref[:, p * LANES:(p + 1) * LANES] = o_scr[p * seq:(p + 1) * seq, :]


def _sb_attn(q, k, v, k_meta, v_meta, batch, seq):
    n_q = seq // TQ
    n_groups = SB_HEADS * SB_HEAD_DIM // (LANES * SB_PAIRS)
    width = LANES * SB_PAIRS
    sched = _sb_schedule(n_q, seq)
    n_tiles = sched.shape[1] - 2 * (SB_STAGES - 1)
    assert (n_tiles + SB_STAGES - 1) % SB_UNROLL == 0
    tbl = jnp.asarray(sched.reshape(-1))

    r = np.arange(TQ)[:, None]
    c = np.arange(TK)[None, :]
    bias = np.zeros((3, TQ, TK), np.float32)
    bias[BIAS_DIAG] = np.where(c < r, 0.0, MASKED)
    bias[BIAS_META] = np.where(c < N_META, 0.0, MASKED) + 0.0 * r
    tri = (np.arange(TK)[:, None] >= c).astype(np.float32)
    pad = ((0, TK - N_META), (0, 0))
    k_meta = jnp.pad(k_meta, pad)
    v_meta = jnp.pad(v_meta, pad)

    blk = pl.BlockSpec((seq, width), lambda b, p, t: (b, p))
    meta_spec = pl.BlockSpec((TK, width), lambda b, p, t: (0, p))
    grid_spec = pltpu.PrefetchScalarGridSpec(
        num_scalar_prefetch=1,
        grid=(batch, n_groups),
        in_specs=[blk, blk, blk, meta_spec, meta_spec,
                  pl.BlockSpec((3, TQ, TK), lambda b, p, t: (0, 0, 0)),
                  pl.BlockSpec((TK, TK), lambda b, p, t: (0, 0))],
        out_specs=blk,
        scratch_shapes=[
            pltpu.VMEM((2, SB_PAIRS * seq, LANES), BF16),
            pltpu.VMEM((SB_PAIRS * (seq + TK), LANES), BF16),
            pltpu.VMEM((2, SB_PAIRS * (seq + TK), LANES), BF16),
            pltpu.VMEM((SB_PAIRS * seq, LANES), BF16),
            pltpu.VMEM((4, 2, TQ, TK), F32),
            pltpu.VMEM((2, 2, TQ, TK), BF16),
            pltpu.VMEM((2, 2, TQ, TK), F32),
            pltpu.VMEM((2, TQ, 2 * TK), BF16),
            pltpu.VMEM((3, TQ, LANES), F32),
            pltpu.VMEM((2, TQ, LANES), F32),
        ])
    return pl.pallas_call(
        functools.partial(_sb_attn_kernel, n_tiles=n_tiles),
        grid_spec=grid_spec,
        out_shape=jax.ShapeDtypeStruct(q.shape, BF16),
        compiler_params=pltpu.CompilerParams(
            dimension_semantics=("arbitrary", "arbitrary"),
            vmem_limit_bytes=VMEM_LIMIT_BYTES),
        name="sb_attn",
    )(tbl, q, k, v, k_meta, v_meta, jnp.asarray(bias), jnp.asarray(tri, BF16))


def _out_ffn_kernel(x_ref, gc_ref, g1_ref, o_ref, wsb_ref, wo_ref, g2_ref,
                    wgate_ref, wup_ref, wdown_ref, out_ref, h_scr, n2_scr):
    f = pl.program_id(1)

    @pl.when(f == 0)
    def _():
        y_sb = _dot(o_ref[...], wsb_ref[...])
        merged = gc_ref[...].astype(F32) + g1_ref[...].astype(F32) * y_sb
        h = x_ref[...] + _dot(merged.astype(BF16), wo_ref[...])
        h_scr[...] = h
        ms = jnp.mean(h * h, axis=-1, keepdims=True)
        n2_scr[...] = (h * lax.rsqrt(ms + EPS) * g2_ref[...]).astype(BF16)

    n2 = n2_scr[...]
    gate = _dot(n2, wgate_ref[...])
    up = _dot(n2, wup_ref[...])
    mid = (gate * jax.nn.sigmoid(gate) * up).astype(BF16)
    h_scr[...] += _dot(mid, wdown_ref[...])

    @pl.when(f == pl.num_programs(1) - 1)
    def _():
        out_ref[...] = h_scr[...]


def _out_ffn(x2d, gc, g1, o, w_sb_out, w_o, norm2_g, w_ffn_in, w_ffn_out):
    rows = x2d.shape[0]
    tm = TM_OUT
    n_f = D_FF // TF
    row_spec = pl.BlockSpec((tm, D_MODEL), lambda i, f: (i, 0))
    once = pl.Buffered(1)
    sq_spec = pl.BlockSpec((D_MODEL, D_MODEL), lambda i, f: (0, 0), pipeline_mode=once)
    return pl.pallas_call(
        _out_ffn_kernel,
        grid=(rows // tm, n_f),
        in_specs=[row_spec, row_spec, row_spec, row_spec, sq_spec, sq_spec,
                  pl.BlockSpec((1, D_MODEL), lambda i, f: (0, 0), pipeline_mode=once),
                  pl.BlockSpec((D_MODEL, TF), lambda i, f: (0, f)),
                  pl.BlockSpec((D_MODEL, TF), lambda i, f: (0, n_f + f)),
                  pl.BlockSpec((TF, D_MODEL), lambda i, f: (f, 0))],
        out_specs=row_spec,
        out_shape=jax.ShapeDtypeStruct((rows, D_MODEL), F32),
        scratch_shapes=[pltpu.VMEM((tm, D_MODEL), F32), pltpu.VMEM((tm, D_MODEL), BF16)],
        compiler_params=pltpu.CompilerParams(
            dimension_semantics=("arbitrary", "arbitrary"),
            vmem_limit_bytes=VMEM_LIMIT_BYTES),
        name="out_ffn",
    )(x2d, gc, g1, o, w_sb_out, w_o, norm2_g, w_ffn_in, w_ffn_in, w_ffn_out)


def kernel(x, meta_tokens, norm1_g, w_in, b_gate, conv_w, w_conv_out, q_norm_g, k_norm_g,
           w_sb_out, w_o, norm2_g, w_ffn_in, w_ffn_out):
    batch, seq, d = x.shape
    assert d == D_MODEL and seq % TQ == 0 and seq % TM_IN == 0 and norm1_g.shape[0] == 1
    x2d = x.reshape(batch * seq, d)

    n_col = D_MODEL // TN_IN
    w_in_b = (w_in[0].astype(BF16).reshape(D_MODEL, N_GROUPS, n_col, TN_IN)
              .transpose(0, 2, 1, 3).reshape(D_MODEL, N_GROUPS * D_MODEL))
    w_co_b = w_conv_out[0].astype(BF16)
    reps = D_MODEL // SB_HEAD_DIM
    q_gain = jnp.tile(q_norm_g[0], reps)[None, :]
    k_gain = jnp.tile(k_norm_g[0], reps)[None, :]
    head = jnp.arange(TN_IN) // SB_HEAD_DIM
    seg = jnp.where(head[:, None] == head[None, :], 1.0 / SB_HEAD_DIM, 0.0).astype(BF16)
    proj_args = (norm1_g, w_in_b, b_gate, conv_w[0], w_co_b, q_gain, k_gain, seg)

    zero_halo = jnp.zeros((SUBLANES, D_MODEL), F32)
    _, _, k_meta, v_meta, _, u_tail = _in_proj(
        meta_tokens.astype(F32), N_META, 1, zero_halo, *proj_args)

    gc, q, k, v, g1, _ = _in_proj(x2d, TM_IN, seq // TM_IN, u_tail, *proj_args)
    o = _sb_attn(q, k, v, k_meta, v_meta, batch, seq)
    out = _out_ffn(x2d, gc, g1, o, w_sb_out[0].astype(BF16), w_o[0].astype(BF16), norm2_g,
                   w_ffn_in[0].astype(BF16), w_ffn_out[0].astype(BF16))
    return out.reshape(batch, seq, d)
```
